```python
import jax, jax.numpy as jnp
from jax import lax
import numpy as np

D_MODEL = 1024
BATCH = 1
SEQ = 16384
DEPTH = 2
DEC_BATCH = 32
DEC_SEQ = 4
PAST_LEN = 16384
PAGE_SIZE = 128

N_GROUPS = 4
D_GROUP = D_MODEL // N_GROUPS
D_IN = 8 * D_GROUP
POOL_WINDOWS = (2, 4, 8, 16)
POOL_GC = D_GROUP // len(POOL_WINDOWS)
POOL_PAD = max(POOL_WINDOWS) - 1
GMLP_HEADS = 4
GMLP_HEAD_DIM = D_GROUP // GMLP_HEADS
CHUNK = 128
SB_HEADS = 4
SB_HEAD_DIM = D_GROUP // SB_HEADS
SB_SCALE = SB_HEAD_DIM ** -0.5
SB_BIAS_INIT = -6.0
Q_BLOCK = 128
CONV_W = 31
CONV_PAD = CONV_W - 1
N_MEM = 256
MEM_HEADS = 4
MEM_HEAD_DIM = D_MODEL // MEM_HEADS
N_EXPERTS = 32
TOP_K = 4
D_FF = D_MODEL
SWIGLU_LIMIT = 7.0
SWIGLU_ALPHA = 1.702
MOE_BLOCK = 128
LN_EPS = 1e-5
DEEPNORM_ALPHA = (2 * DEPTH) ** 0.25
DEEPNORM_BETA = (8 * DEPTH) ** -0.25

kernel_name = "hymba_pool_gmlp_stickbreak_conformer_moe_decode"


def layer_norm(x, g, b):
    xf = x.astype(jnp.float32)
    mu = jnp.mean(xf, axis=-1, keepdims=True)
    var = jnp.mean(jnp.square(xf - mu), axis=-1, keepdims=True)
    return ((xf - mu) * lax.rsqrt(var + LN_EPS) * g + b).astype(x.dtype)


def pool_mixer(p_ext, pos, pool_w, pool_scale):
    B, L, _ = p_ext.shape
    T = L - POOL_PAD
    pf = p_ext.astype(jnp.float32)
    cs = jnp.concatenate([jnp.zeros((B, 1, D_GROUP), jnp.float32), jnp.cumsum(pf, axis=1)], axis=1)
    end = cs[:, POOL_PAD + 1:]
    cur = pf[:, POOL_PAD:]
    outs = []
    for g, w in enumerate(POOL_WINDOWS):
        ch = slice(g * POOL_GC, (g + 1) * POOL_GC)
        start = cs[:, POOL_PAD + 1 - w:POOL_PAD + 1 - w + T, ch]
        cnt = jnp.minimum(pos + 1, w).astype(jnp.float32)[None, :, None]
        outs.append((end[..., ch] - start) / cnt - cur[..., ch])
    d = jnp.stack(outs, axis=2)
    y = jnp.einsum('btgc,gcd->btgd', d, pool_w.astype(jnp.float32)).reshape(B, T, D_GROUP)
    return (y * pool_scale).astype(p_ext.dtype)


def gmlp_mixer(u, v, w_s, b_s):
    B, T, _ = v.shape
    n_ch = -(-T // CHUNK)
    vp = jnp.pad(v, ((0, 0), (0, n_ch * CHUNK - T), (0, 0)))
    vp = vp.reshape(B, n_ch, CHUNK, GMLP_HEADS, GMLP_HEAD_DIM)
    causal = jnp.tril(jnp.ones((CHUNK, CHUNK), dtype=bool))
    w = jnp.where(causal[None], w_s, 0.0).astype(v.dtype)
    s = jnp.einsum('hts,bcshd->bcthd', w, vp) + b_s.T[None, None, :, :, None]
    s = s.reshape(B, n_ch * CHUNK, D_GROUP)[:, :T]
    return u * s.astype(u.dtype)


def stick_breaking_attention(q, k, v, q_pos, k_pos, sb_bias):
    B, T, H, Dh = q.shape
    blk = min(Q_BLOCK, T)
    nb = T // blk
    qb = q.reshape(B, nb, blk, H, Dh).transpose(1, 0, 2, 3, 4)
    pb = q_pos.reshape(nb, blk)
    bias = sb_bias.astype(jnp.float32)[None, :, None, None]

    def one_block(args):
        q_blk, qp = args
        z = jnp.einsum('bqhd,bshd->bhqs', q_blk, k).astype(jnp.float32) * SB_SCALE + bias
        causal = (k_pos[None, :] < qp[:, None])[None, None]
        log_not = jnp.where(causal, jax.nn.log_sigmoid(-z), 0.0)
        rest = lax.cumsum(log_not, axis=3, reverse=True) - log_not
        w = jnp.where(causal, jnp.exp(jax.nn.log_sigmoid(z) + rest), 0.0)
        return jnp.einsum('bhqs,bshd->bqhd', w.astype(v.dtype), v)

    o = lax.map(one_block, (qb, pb))
    return o.transpose(1, 0, 2, 3, 4).reshape(B, T, H * Dh)


def conv_mixer(g_ext, conv_w, conv_b, ln_g, ln_b):
    y = lax.conv_general_dilated(
        g_ext, conv_w.astype(g_ext.dtype)[:, None, :], window_strides=(1,), padding='VALID',
        dimension_numbers=('NWC', 'WIO', 'NWC'), feature_group_count=D_GROUP)
    y = layer_norm(y + conv_b, ln_g, ln_b)
    return jax.nn.silu(y)


def memory_attention(x, mem_k, mem_v, w_q, w_o):
    B, T, _ = x.shape
    q = (x @ w_q).reshape(B, T, MEM_HEADS, MEM_HEAD_DIM)
    s = jnp.einsum('bthd,bmhd->bhtm', q, mem_k).astype(jnp.float32) * (MEM_HEAD_DIM ** -0.5)
    p = jax.nn.softmax(s, axis=-1).astype(x.dtype)
    o = jnp.einsum('bhtm,bmhd->bthd', p, mem_v).reshape(B, T, D_MODEL)
    return o @ w_o


def moe_ffn(x, router_w, router_b, w_up, b_up, w_down, b_down):
    B, T, D = x.shape
    xt = x.reshape(B * T, D)
    N = xt.shape[0]
    logits = (xt @ router_w).astype(jnp.float32) + router_b
    top_val, top_idx = lax.top_k(logits, TOP_K)
    gates = jax.nn.softmax(top_val, axis=-1)
    NK = N * TOP_K
    flat_e = top_idx.reshape(NK)
    order = jnp.argsort(flat_e)
    e_sorted = flat_e[order]
    tok_sorted = (order // TOP_K).astype(jnp.int32)
    g_sorted = gates.reshape(NK)[order]
    counts = jnp.bincount(flat_e, length=N_EXPERTS)
    padded = (counts + MOE_BLOCK - 1) // MOE_BLOCK * MOE_BLOCK
    start = jnp.cumsum(counts) - counts
    pend = jnp.cumsum(padded)
    pstart = pend - padded
    dest = pstart[e_sorted] + (jnp.arange(NK) - start[e_sorted])
    n_blocks = (NK + MOE_BLOCK - 1) // MOE_BLOCK + N_EXPERTS
    n_slots = n_blocks * MOE_BLOCK
    slot_tok = jnp.full((n_slots,), N, jnp.int32).at[dest].set(tok_sorted)
    slot_gate = jnp.zeros((n_slots,), jnp.float32).at[dest].set(g_sorted)
    block_expert = jnp.minimum(
        jnp.searchsorted(pend, jnp.arange(n_blocks) * MOE_BLOCK, side='right'), N_EXPERTS - 1)
    x_pad = jnp.concatenate([xt, jnp.zeros((1, D), xt.dtype)], axis=0)

    def run_block(args):
        toks, e = args
        xb = x_pad[toks]
        hb = xb @ w_up[e] + b_up[e]
        g, lin = hb[:, :D_FF], hb[:, D_FF:]
        g = jnp.minimum(g, SWIGLU_LIMIT)
        lin = jnp.clip(lin, -SWIGLU_LIMIT, SWIGLU_LIMIT)
        a = g * jax.nn.sigmoid(SWIGLU_ALPHA * g) * (lin + 1.0)
        return a @ w_down[e] + b_down[e]

    y_blocks = lax.map(run_block, (slot_tok.reshape(n_blocks, MOE_BLOCK), block_expert))
    y_slots = y_blocks.reshape(n_slots, D) * slot_gate[:, None].astype(x.dtype)
    y = jax.ops.segment_sum(y_slots, slot_tok, num_segments=N + 1)[:N]
    return y.reshape(B, T, D)


def trunk_layer(x, pos, pool_prev, conv_prev, k_past, v_past, mem_k, mem_v, params):
    (w_in, w_o, pool_w, pool_scale, gmlp_ln_g, gmlp_ln_b, gmlp_ws, gmlp_bs, sb_bias, conv_w, conv_b,
     conv_ln_g, conv_ln_b, mem_wq, mem_wo, ln_g, ln_b, router_w, router_b,
     expert_w_up, expert_b_up, expert_w_down, expert_b_down) = params
    B, T, _ = x.shape
    h = x @ w_in
    p_in, u_in, v_in, q_in, k_in, sv_in, a_in, gate_in = jnp.split(h, 8, axis=-1)
    p_ext = jnp.concatenate([pool_prev, p_in], axis=1)
    y_a = pool_mixer(p_ext, pos, pool_w, pool_scale)
    v_n = layer_norm(jax.nn.gelu(v_in), gmlp_ln_g, gmlp_ln_b)
    y_b = gmlp_mixer(jax.nn.gelu(u_in), v_n, gmlp_ws, gmlp_bs)
    q = q_in.reshape(B, T, SB_HEADS, SB_HEAD_DIM)
    k = k_in.reshape(B, T, SB_HEADS, SB_HEAD_DIM)
    sv = sv_in.reshape(B, T, SB_HEADS, SB_HEAD_DIM)
    k_all = jnp.concatenate([k_past, k], axis=1)
    v_all = jnp.concatenate([v_past, sv], axis=1)
    y_c = stick_breaking_attention(q, k_all, v_all, pos, jnp.arange(k_all.shape[1]), sb_bias)
    glu = a_in * jax.nn.sigmoid(gate_in)
    g_ext = jnp.concatenate([conv_prev, glu], axis=1)
    y_d = conv_mixer(g_ext, conv_w, conv_b, conv_ln_g, conv_ln_b)
    mix = jnp.concatenate([y_a, y_b, y_c, y_d], axis=-1) @ w_o
    x = layer_norm(DEEPNORM_ALPHA * x + mix, ln_g[0], ln_b[0])
    x = layer_norm(DEEPNORM_ALPHA * x + memory_attention(x, mem_k, mem_v, mem_wq, mem_wo), ln_g[1], ln_b[1])
    ff = moe_ffn(x, router_w, router_b, expert_w_up, expert_b_up, expert_w_down, expert_b_down)
    x = layer_norm(DEEPNORM_ALPHA * x + ff, ln_g[2], ln_b[2])
    return x, k, sv, p_ext[:, -POOL_PAD:], g_ext[:, -CONV_PAD:], v_n


def setup_inputs(seed: int = 0) -> dict:
    key = jax.random.key(seed)
    ks = jax.random.split(key, 40)
    n_pages = PAST_LEN // PAGE_SIZE
    n_phys = (DEC_BATCH * n_pages * 5) // 4

    def nrm(k, shape, s):
        return jax.random.normal(k, shape, jnp.float32) * s

    page_table = jax.random.permutation(ks[0], n_phys)[:DEC_BATCH * n_pages]
    page_table = page_table.reshape(DEC_BATCH, n_pages).astype(jnp.int32)
    return {
        "x_prompt": nrm(ks[1], (BATCH, SEQ, D_MODEL), 1.0),
        "x_sample": nrm(ks[2], (DEC_BATCH, DEC_SEQ, D_MODEL), 1.0),
        "cache_sb_k": nrm(ks[3], (DEPTH, n_phys, PAGE_SIZE, SB_HEADS, SB_HEAD_DIM), 1.0),
        "cache_sb_v": nrm(ks[4], (DEPTH, n_phys, PAGE_SIZE, SB_HEADS, SB_HEAD_DIM), 1.0),
        "state_pool": nrm(ks[5], (DEPTH, DEC_BATCH, POOL_PAD, D_GROUP), 1.0),
        "state_conv": nrm(ks[6], (DEPTH, DEC_BATCH, CONV_PAD, D_GROUP), 0.5),
        "cache_mem_k": nrm(ks[7], (DEPTH, DEC_BATCH, N_MEM, MEM_HEADS, MEM_HEAD_DIM), 1.0),
        "cache_mem_v": nrm(ks[8], (DEPTH, DEC_BATCH, N_MEM, MEM_HEADS, MEM_HEAD_DIM), DEEPNORM_BETA),
        "page_table": page_table,
        "mem_prompt": nrm(ks[9], (BATCH, N_MEM, D_MODEL), 1.0),
        "w_in": nrm(ks[10], (DEPTH, D_MODEL, D_IN), D_MODEL ** -0.5),
        "w_o": nrm(ks[11], (DEPTH, D_MODEL, D_MODEL), D_MODEL ** -0.5 * DEEPNORM_BETA),
        "pool_w": nrm(ks[12], (DEPTH, len(POOL_WINDOWS), POOL_GC, POOL_GC), POOL_GC ** -0.5),
        "pool_scale": 1.0 + nrm(ks[13], (DEPTH, D_GROUP), 0.1),
        "gmlp_ln_g": 1.0 + nrm(ks[14], (DEPTH, D_GROUP), 0.05),
        "gmlp_ln_b": nrm(ks[15], (DEPTH, D_GROUP), 0.02),
        "gmlp_ws": nrm(ks[16], (DEPTH, GMLP_HEADS, CHUNK, CHUNK), CHUNK ** -0.5),
        "gmlp_bs": 1.0 + nrm(ks[17], (DEPTH, GMLP_HEADS, CHUNK), 0.02),
        "sb_bias": SB_BIAS_INIT + nrm(ks[34], (DEPTH, SB_HEADS), 0.1),
        "conv_w": nrm(ks[18], (DEPTH, CONV_W, D_GROUP), CONV_W ** -0.5),
        "conv_b": nrm(ks[19], (DEPTH, D_GROUP), 0.02),
        "conv_ln_g": 1.0 + nrm(ks[20], (DEPTH, D_GROUP), 0.05),
        "conv_ln_b": nrm(ks[21], (DEPTH, D_GROUP), 0.02),
        "mem_wq": nrm(ks[22], (DEPTH, D_MODEL, D_MODEL), D_MODEL ** -0.5),
        "mem_wk": nrm(ks[23], (DEPTH, D_MODEL, D_MODEL), D_MODEL ** -0.5),
        "mem_wv": nrm(ks[24], (DEPTH, D_MODEL, D_MODEL), D_MODEL ** -0.5 * DEEPNORM_BETA),
        "mem_wo": nrm(ks[25], (DEPTH, D_MODEL, D_MODEL), D_MODEL ** -0.5 * DEEPNORM_BETA),
        "ln_g": 1.0 + nrm(ks[26], (DEPTH, 3, D_MODEL), 0.05),
        "ln_b": nrm(ks[27], (DEPTH, 3, D_MODEL), 0.02),
        "router_w": nrm(ks[28], (DEPTH, D_MODEL, N_EXPERTS), D_MODEL ** -0.5),
        "router_b": nrm(ks[29], (DEPTH, N_EXPERTS), 0.01),
        "expert_w_up": nrm(ks[30], (DEPTH, N_EXPERTS, D_MODEL, 2 * D_FF), D_MODEL ** -0.5),
        "expert_b_up": nrm(ks[31], (DEPTH, N_EXPERTS, 2 * D_FF), 0.02),
        "expert_w_down": nrm(ks[32], (DEPTH, N_EXPERTS, D_FF, D_MODEL), D_FF ** -0.5 * DEEPNORM_BETA),
        "expert_b_down": nrm(ks[33], (DEPTH, N_EXPERTS, D_MODEL), 0.02),
    }


def reference(x_prompt, x_sample, cache_sb_k, cache_sb_v, state_pool, state_conv, cache_mem_k,
              cache_mem_v, page_table, mem_prompt, w_in, w_o, pool_w, pool_scale, gmlp_ln_g,
              gmlp_ln_b, gmlp_ws, gmlp_bs, sb_bias, conv_w, conv_b, conv_ln_g, conv_ln_b, mem_wq,
              mem_wk, mem_wv, mem_wo, ln_g, ln_b, router_w, router_b, expert_w_up, expert_b_up,
              expert_w_down, expert_b_down):
    B_p, T_p, _ = x_prompt.shape
    B_s, T_s, _ = x_sample.shape
    past_len = page_table.shape[1] * PAGE_SIZE
    pos_p = jnp.arange(T_p)
    pos_s = past_len + jnp.arange(T_s)
    dt = x_prompt.dtype
    pool_zero = jnp.zeros((B_p, POOL_PAD, D_GROUP), dt)
    conv_zero = jnp.zeros((B_p, CONV_PAD, D_GROUP), dt)
    kv_empty = jnp.zeros((B_p, 0, SB_HEADS, SB_HEAD_DIM), dt)

    xp, xs = x_prompt, x_sample
    sbk_p, sbv_p, sbk_s, sbv_s = [], [], [], []
    pool_p, pool_s, conv_p, conv_s, gv_s, mk_p, mv_p = [], [], [], [], [], [], []
    for l in range(DEPTH):
        params = (w_in[l], w_o[l], pool_w[l], pool_scale[l], gmlp_ln_g[l], gmlp_ln_b[l], gmlp_ws[l],
                  gmlp_bs[l], sb_bias[l], conv_w[l], conv_b[l], conv_ln_g[l], conv_ln_b[l], mem_wq[l],
                  mem_wo[l], ln_g[l], ln_b[l], router_w[l], router_b[l], expert_w_up[l], expert_b_up[l],
                  expert_w_down[l], expert_b_down[l])
        mem_k = (mem_prompt @ mem_wk[l]).reshape(B_p, N_MEM, MEM_HEADS, MEM_HEAD_DIM)
        mem_v = (mem_prompt @ mem_wv[l]).reshape(B_p, N_MEM, MEM_HEADS, MEM_HEAD_DIM)
        xp, k_new, v_new, pool_new, conv_new, _ = trunk_layer(
            xp, pos_p, pool_zero, conv_zero, kv_empty, kv_empty, mem_k, mem_v, params)
        sbk_p.append(k_new); sbv_p.append(v_new); pool_p.append(pool_new); conv_p.append(conv_new)
        mk_p.append(mem_k); mv_p.append(mem_v)
        k_past = cache_sb_k[l][page_table].reshape(B_s, past_len, SB_HEADS, SB_HEAD_DIM)
        v_past = cache_sb_v[l][page_table].reshape(B_s, past_len, SB_HEADS, SB_HEAD_DIM)
        xs, k_new, v_new, pool_new, conv_new, gv_new = trunk_layer(
            xs, pos_s, state_pool[l], state_conv[l], k_past, v_past, cache_mem_k[l], cache_mem_v[l], params)
        sbk_s.append(k_new); sbv_s.append(v_new); pool_s.append(pool_new); conv_s.append(conv_new)
        gv_s.append(gv_new)

    return (xp, xs,
            jnp.stack(sbk_p), jnp.stack(sbv_p), jnp.stack(sbk_s), jnp.stack(sbv_s),
            jnp.stack(pool_p), jnp.stack(pool_s), jnp.stack(conv_p), jnp.stack(conv_s),
            jnp.stack(gv_s), jnp.stack(mk_p), jnp.stack(mv_p))
```

```python
import functools

import numpy as np
import jax
import jax.numpy as jnp
from jax import lax
from jax.experimental import pallas as pl
from jax.experimental.pallas import tpu as pltpu

F32 = jnp.float32
BF16 = jnp.bfloat16
I32 = jnp.int32

D_MODEL = 1024
D_GROUP = 256
N_HEADS = 4
HEAD_DIM = D_GROUP // N_HEADS
POOL_WINDOWS = (2, 4, 8, 16)
POOL_PAD = 15
CONV_W = 31
CONV_PAD = 30
CHUNK = 128
PAGE = 128
N_MEM = 256
MEM_HEAD_DIM = D_MODEL // N_HEADS
N_EXPERTS = 32
TOP_K = 4
D_FF = 1024
SWIGLU_LIMIT = 7.0
SWIGLU_ALPHA = 1.702
LN_EPS = 1e-5
DEPTH = 2
DEEPNORM_ALPHA = (2 * DEPTH) ** 0.25

POOL_ROW0 = 16
CONV_ROW0 = 32
SB_SUB = 256
EXPERT_BLOCK = 256
VMEM_LIMIT = 56 * 1024 * 1024

NT_DIMS = (((1,), (1,)), ((), ()))


def _pick(n, cands):
    for c in cands:
        if n % c == 0:
            return c
    raise ValueError(f"no tile for {n} in {cands}")


def _cparams(sem):
    return pltpu.CompilerParams(dimension_semantics=sem, vmem_limit_bytes=VMEM_LIMIT)


def _ln(x, g, b):
    mu = jnp.mean(x, axis=-1, keepdims=True)
    xc = x - mu
    var = jnp.mean(xc * xc, axis=-1, keepdims=True)
    return xc * lax.rsqrt(var + LN_EPS) * g + b


def _sigmoid(x):
    return 1.0 / (1.0 + jnp.exp(-x))


def _gelu(x):
    return 0.5 * x * (1.0 + jnp.tanh(0.7978845608028654 * (x + 0.044715 * (x * x * x))))


def _head_of_lane(shape, axis):
    return lax.broadcasted_iota(I32, shape, axis) // HEAD_DIM


def _in_proj_body(x_ref, w_ref, g_ref, b_ref, p_ref, gu_ref, vn_ref, q_ref, k_ref, sv_ref, kb_ref,
                  vb_ref, glu_ref):
    G = D_GROUP
    h = jnp.dot(x_ref[...].astype(BF16), w_ref[...], preferred_element_type=F32)
    p_ref[...] = h[:, 0:G]
    gu_ref[...] = _gelu(h[:, G:2 * G])
    vn_ref[...] = _ln(_gelu(h[:, 2 * G:3 * G]), g_ref[...], b_ref[...])
    q_ref[...] = h[:, 3 * G:4 * G] * (HEAD_DIM ** -0.5)
    k = h[:, 4 * G:5 * G]
    k_ref[...] = k
    kb_ref[...] = k.astype(BF16)
    sv = h[:, 5 * G:6 * G]
    sv_ref[...] = sv
    vb_ref[...] = sv.astype(BF16)
    glu_ref[...] = h[:, 6 * G:7 * G] * _sigmoid(h[:, 7 * G:8 * G])


def _in_proj(x, w_bf, ln_g, ln_b):
    n = x.shape[0]
    tm = _pick(n, (512, 256, 128, 64, 32, 16, 8))
    G = D_GROUP
    row = lambda i: (i, 0)
    fix = lambda i: (0, 0)
    f32o = jax.ShapeDtypeStruct((n, G), F32)
    bfo = jax.ShapeDtypeStruct((n, G), BF16)
    blk = pl.BlockSpec((tm, G), row)
    return pl.pallas_call(
        _in_proj_body,
        out_shape=(f32o, f32o, f32o, f32o, f32o, f32o, bfo, bfo, f32o),
        grid=(n // tm,),
        in_specs=[pl.BlockSpec((tm, D_MODEL), row), pl.BlockSpec((D_MODEL, 8 * G), fix),
                  pl.BlockSpec((1, G), fix), pl.BlockSpec((1, G), fix)],
        out_specs=(blk,) * 9,
        compiler_params=_cparams(("arbitrary",)),
        name="in_proj",
    )(x, w_bf, ln_g, ln_b)


def _mixer_body(sp_ref, sc_ref, p_ref, gu_ref, vn_ref, glu_ref, pw_ref, ps_ref, ws_ref, bst_ref,
                cw_ref, cb_ref, cg_ref, cbeta_ref, y_ref, po_ref, co_ref, pext, gext, *, pos0, tb):
    G = D_GROUP
    i = pl.program_id(1)

    @pl.when(i == 0)
    def _():
        pext[1:POOL_ROW0, :] = sp_ref[0]
        gext[2:CONV_ROW0, :] = sc_ref[0]

    @pl.when(i > 0)
    def _():
        pext[1:POOL_ROW0, :] = pext[tb + 1:tb + POOL_ROW0, :]
        gext[2:CONV_ROW0, :] = gext[tb + 2:tb + CONV_ROW0, :]

    pext[POOL_ROW0:POOL_ROW0 + tb, :] = p_ref[0]
    gext[CONV_ROW0:CONV_ROW0 + tb, :] = glu_ref[0]

    rc = min(tb, CHUNK)
    lane = lax.broadcasted_iota(I32, (rc, G), 1)
    rowi = lax.broadcasted_iota(I32, (rc, G), 0)
    wl = jnp.where(lane < 64, 2.0, jnp.where(lane < 128, 4.0, jnp.where(lane < 192, 8.0, 16.0)))
    if rc == CHUNK:
        tr = lax.broadcasted_iota(I32, (CHUNK, CHUNK), 0)
        tc = lax.broadcasted_iota(I32, (CHUNK, CHUNK), 1)
        w_tril = [jnp.where(tr >= tc, ws_ref[h], 0.0).astype(BF16) for h in range(N_HEADS)]
        hol = _head_of_lane((rc, G), 1)

    for c in range(tb // rc):
        r0 = c * rc
        cur = pext[POOL_ROW0 + r0:POOL_ROW0 + r0 + rc, :]
        acc = cur
        snaps = {}
        for j in range(1, max(POOL_WINDOWS)):
            acc = acc + pext[POOL_ROW0 + r0 - j:POOL_ROW0 + r0 - j + rc, :]
            if j + 1 in POOL_WINDOWS:
                snaps[j + 1] = acc
        win = jnp.where(lane < 64, snaps[2], jnp.where(lane < 128, snaps[4],
                                                       jnp.where(lane < 192, snaps[8], snaps[16])))
        pos = pos0 + i * tb + r0 + rowi
        cnt = jnp.minimum((pos + 1).astype(F32), wl)
        d = win / cnt - cur
        ya = jnp.dot(d.astype(BF16), pw_ref[...], preferred_element_type=F32) * ps_ref[...]
        vn = vn_ref[0, r0:r0 + rc, :]
        s = bst_ref[0:rc, :]
        if rc == CHUNK:
            vb = vn.astype(BF16)
            for h in range(N_HEADS):
                r = jnp.dot(w_tril[h], vb, preferred_element_type=F32)
                s = s + jnp.where(hol == h, r, 0.0)
        else:
            for j in range(rc):
                s = s + jnp.where(rowi >= j, ws_ref[:, j, :], 0.0) * vn[j:j + 1, :]
        yb = gu_ref[0, r0:r0 + rc, :] * s
        cacc = jnp.zeros((rc, G), F32) + cb_ref[...]
        for j in range(CONV_W):
            cacc = cacc + cw_ref[j:j + 1, :] * gext[2 + r0 + j:2 + r0 + j + rc, :]
        yd = _ln(cacc, cg_ref[...], cbeta_ref[...])
        yd = yd * _sigmoid(yd)
        y_ref[0, r0:r0 + rc, 0:G] = ya.astype(y_ref.dtype)
        y_ref[0, r0:r0 + rc, G:2 * G] = yb.astype(y_ref.dtype)
        y_ref[0, r0:r0 + rc, 2 * G:3 * G] = yd.astype(y_ref.dtype)

    po_ref[0] = pext[tb + 1:tb + POOL_ROW0, :]
    co_ref[0] = gext[tb + 2:tb + CONV_ROW0, :]


def _mixers(state_p, state_c, p, gu, vn, glu, pool_bd, pool_scale, ws, bs_tile, conv_w, conv_b, cg,
            cbeta, pos0):
    B, T, G = p.shape
    tb = _pick(T, (512, 256, 128)) if T % CHUNK == 0 else T
    nt = T // tb
    seq = lambda b, i: (b, i, 0)
    per_b = lambda b, i: (b, 0, 0)
    fix2 = lambda b, i: (0, 0)
    fix3 = lambda b, i: (0, 0, 0)
    ws_spec = pl.BlockSpec(ws.shape, fix3)
    return pl.pallas_call(
        functools.partial(_mixer_body, pos0=pos0, tb=tb),
        out_shape=(jax.ShapeDtypeStruct((B, T, 3 * G), BF16 if tb % 16 == 0 else F32),
                   jax.ShapeDtypeStruct((B, POOL_PAD, G), F32),
                   jax.ShapeDtypeStruct((B, CONV_PAD, G), F32)),
        grid=(B, nt),
        in_specs=[pl.BlockSpec((1, POOL_PAD, G), per_b), pl.BlockSpec((1, CONV_PAD, G), per_b),
                  pl.BlockSpec((1, tb, G), seq), pl.BlockSpec((1, tb, G), seq),
                  pl.BlockSpec((1, tb, G), seq), pl.BlockSpec((1, tb, G), seq),
                  pl.BlockSpec((G, G), fix2), pl.BlockSpec((1, G), fix2), ws_spec,
                  pl.BlockSpec(bs_tile.shape, fix2), pl.BlockSpec((CONV_W, G), fix2),
                  pl.BlockSpec((1, G), fix2), pl.BlockSpec((1, G), fix2), pl.BlockSpec((1, G), fix2)],
        out_specs=(pl.BlockSpec((1, tb, 3 * G), seq), pl.BlockSpec((1, POOL_PAD, G), per_b),
                   pl.BlockSpec((1, CONV_PAD, G), per_b)),
        scratch_shapes=[pltpu.VMEM((POOL_ROW0 + tb, G), F32), pltpu.VMEM((CONV_ROW0 + tb, G), F32)],
        compiler_params=_cparams(("arbitrary", "arbitrary")),
        name="mixers",
    )(state_p, state_c, p, gu, vn, glu, pool_bd, pool_scale, ws, bs_tile, conv_w, conv_b, cg, cbeta)


def _sb_block(qs, kb, vb, mask, bias, tri, c_ref, acc_ref):
    z = lax.dot_general(qs, kb, NT_DIMS, preferred_element_type=F32) + bias
    sp = jnp.maximum(z, 0.0) + jnp.log(1.0 + jnp.exp(-jnp.abs(z)))
    if mask is not None:
        sp = jnp.where(mask, sp, 0.0)
    hi = sp.astype(BF16)
    lo = (sp - hi.astype(F32)).astype(BF16)
    s_loc = (jnp.dot(hi, tri, preferred_element_type=F32)
             + jnp.dot(lo, tri, preferred_element_type=F32))
    c = c_ref[...]
    w = jnp.exp(z - (s_loc + c))
    if mask is not None:
        w = jnp.where(mask, w, 0.0)
    acc_ref[...] += jnp.dot(w.astype(BF16), vb, preferred_element_type=F32)
    c_ref[...] = c + jnp.sum(sp, axis=-1, keepdims=True)


def _sb_prompt_body(qi_ref, kj_ref, q_ref, k_ref, v_ref, tri_ref, bias_ref, o_ref, qs_ref, c_ref,
                    acc_ref, *, tq, cs):
    step = pl.program_id(0)
    qi = qi_ref[step]
    kj = kj_ref[step]
    R = N_HEADS * tq

    @pl.when(kj == qi)
    def _():
        q = q_ref[...]
        hol = _head_of_lane((tq, D_GROUP), 1)
        for h in range(N_HEADS):
            qs_ref[h * tq:(h + 1) * tq, :] = jnp.where(hol == h, q, 0.0).astype(BF16)
        c_ref[...] = jnp.zeros_like(c_ref)
        acc_ref[...] = jnp.zeros_like(acc_ref)

    def run(masked):
        for u in reversed(range(tq // cs)):
            mask = None
            if masked:
                t = lax.broadcasted_iota(I32, (R, cs), 0) % tq
                col = lax.broadcasted_iota(I32, (R, cs), 1) + u * cs
                mask = col < t
            _sb_block(qs_ref[...], k_ref[u * cs:(u + 1) * cs, :], v_ref[u * cs:(u + 1) * cs, :],
                      mask, bias_ref[...], tri_ref[...], c_ref, acc_ref)

    @pl.when(kj == qi)
    def _():
        run(True)

    @pl.when(kj != qi)
    def _():
        run(False)

    @pl.when(kj == 0)
    def _():
        hol = _head_of_lane((tq, D_GROUP), 1)
        o = jnp.zeros((tq, D_GROUP), F32)
        for h in range(N_HEADS):
            o = o + jnp.where(hol == h, acc_ref[h * tq:(h + 1) * tq, :], 0.0)
        o_ref[...] = o.astype(BF16)


def _sb_prompt(q, kb, vb, sb_bias):
    T = q.shape[0]
    tq = _pick(T, (512, 256, 128))
    cs = min(SB_SUB, tq)
    nq = T // tq
    qi = np.concatenate([np.full(i + 1, i) for i in range(nq)]).astype(np.int32)
    kj = np.concatenate([np.arange(i, -1, -1) for i in range(nq)]).astype(np.int32)
    R = N_HEADS * tq
    tri = (np.arange(cs)[:, None] >= np.arange(cs)[None, :]).astype(np.float32)
    bias = jnp.broadcast_to(jnp.repeat(sb_bias.astype(F32), tq)[:, None], (R, cs))
    grid_spec = pltpu.PrefetchScalarGridSpec(
        num_scalar_prefetch=2,
        grid=(len(qi),),
        in_specs=[pl.BlockSpec((tq, D_GROUP), lambda s, qi, kj: (qi[s], 0)),
                  pl.BlockSpec((tq, D_GROUP), lambda s, qi, kj: (kj[s], 0)),
                  pl.BlockSpec((tq, D_GROUP), lambda s, qi, kj: (kj[s], 0)),
                  pl.BlockSpec((cs, cs), lambda s, qi, kj: (0, 0)),
                  pl.BlockSpec((R, cs), lambda s, qi, kj: (0, 0))],
        out_specs=pl.BlockSpec((tq, D_GROUP), lambda s, qi, kj: (qi[s], 0)),
        scratch_shapes=[pltpu.VMEM((R, D_GROUP), BF16), pltpu.VMEM((R, 1), F32),
                        pltpu.VMEM((R, D_GROUP), F32)],
    )
    return pl.pallas_call(
        functools.partial(_sb_prompt_body, tq=tq, cs=cs),
        out_shape=jax.ShapeDtypeStruct((T, D_GROUP), BF16),
        grid_spec=grid_spec,
        compiler_params=_cparams(("arbitrary",)),
        name="sb_prompt",
    )(jnp.asarray(qi), jnp.asarray(kj), q, kb, vb, jnp.asarray(tri, BF16), bias)


def _sb_sample_body(pt_ref, qs_ref, kn_ref, vn_ref, *rest, pps, n_groups, t_new):
    kp = rest[:pps]
    vp = rest[pps:2 * pps]
    tri_ref, bias_ref, acc_out, c_ref, acc_ref = rest[2 * pps:]
    g = pl.program_id(1)
    R = N_HEADS * t_new

    @pl.when(g == 0)
    def _():
        c_ref[...] = jnp.zeros_like(c_ref)
        acc_ref[...] = jnp.zeros_like(acc_ref)
        t = lax.broadcasted_iota(I32, (R, PAGE), 0) % t_new
        col = lax.broadcasted_iota(I32, (R, PAGE), 1)
        _sb_block(qs_ref[0], kn_ref[0], vn_ref[0], col < t, bias_ref[...], tri_ref[...], c_ref, acc_ref)

    @pl.when(g > 0)
    def _():
        for p in reversed(range(pps)):
            _sb_block(qs_ref[0], kp[p][0].astype(BF16), vp[p][0].astype(BF16), None, bias_ref[...],
                      tri_ref[...], c_ref, acc_ref)

    @pl.when(g == n_groups)
    def _():
        acc_out[0] = acc_ref[...]


def _sb_sample(qs, kn, vn, cache_k, cache_v, page_table, page_base, sb_bias):
    B, R, G = qs.shape
    t_new = R // N_HEADS
    n_pages = page_table.shape[1]
    pps = _pick(n_pages, (8, 4, 2, 1))
    n_groups = n_pages // pps
    tri = (np.arange(PAGE)[:, None] >= np.arange(PAGE)[None, :]).astype(np.float32)
    bias = jnp.broadcast_to(jnp.repeat(sb_bias.astype(F32), t_new)[:, None], (R, PAGE))

    def page_spec(p):
        def imap(b, g, pt):
            grp = n_groups - jnp.maximum(g, 1)
            return (page_base + pt[b, grp * pps + p], 0, 0)
        return pl.BlockSpec((1, PAGE, G), imap)

    per_b = lambda b, g, pt: (b, 0, 0)
    fix = lambda b, g, pt: (0, 0)
    grid_spec = pltpu.PrefetchScalarGridSpec(
        num_scalar_prefetch=1,
        grid=(B, n_groups + 1),
        in_specs=[pl.BlockSpec((1, R, G), per_b), pl.BlockSpec((1, PAGE, G), per_b),
                  pl.BlockSpec((1, PAGE, G), per_b)]
                 + [page_spec(p) for p in range(pps)] + [page_spec(p) for p in range(pps)]
                 + [pl.BlockSpec((PAGE, PAGE), fix), pl.BlockSpec((R, PAGE), fix)],
        out_specs=pl.BlockSpec((1, R, G), per_b),
        scratch_shapes=[pltpu.VMEM((R, 1), F32), pltpu.VMEM((R, G), F32)],
    )
    return pl.pallas_call(
        functools.partial(_sb_sample_body, pps=pps, n_groups=n_groups, t_new=t_new),
        out_shape=jax.ShapeDtypeStruct((B, R, G), F32),
        grid_spec=grid_spec,
        compiler_params=_cparams(("arbitrary", "arbitrary")),
        name="sb_sample",
    )(page_table, qs, kn, vn, *([cache_k] * pps), *([cache_v] * pps), jnp.asarray(tri, BF16), bias)


def _post_mix_body(x_ref, yabd_ref, yc_ref, wabd_ref, wc_ref, g_ref, b_ref, wq_ref, x1_ref, q_ref):
    mix = (jnp.dot(yabd_ref[...].astype(BF16), wabd_ref[...], preferred_element_type=F32)
           + jnp.dot(yc_ref[...], wc_ref[...], preferred_element_type=F32))
    x1 = _ln(DEEPNORM_ALPHA * x_ref[...] + mix, g_ref[...], b_ref[...])
    x1_ref[...] = x1
    q = jnp.dot(x1.astype(BF16), wq_ref[...], preferred_element_type=F32)
    q_ref[...] = (q * (MEM_HEAD_DIM ** -0.5)).astype(BF16)


def _post_mix(x, yabd, yc, wabd, wc, g, b, wq):
    n = x.shape[0]
    tm = _pick(n, (512, 256, 128, 64, 32, 16, 8))
    row = lambda i: (i, 0)
    fix = lambda i: (0, 0)
    return pl.pallas_call(
        _post_mix_body,
        out_shape=(jax.ShapeDtypeStruct((n, D_MODEL), F32), jax.ShapeDtypeStruct((n, D_MODEL), BF16)),
        grid=(n // tm,),
        in_specs=[pl.BlockSpec((tm, D_MODEL), row), pl.BlockSpec((tm, 3 * D_GROUP), row),
                  pl.BlockSpec((tm, D_GROUP), row), pl.BlockSpec((3 * D_GROUP, D_MODEL), fix),
                  pl.BlockSpec((D_GROUP, D_MODEL), fix), pl.BlockSpec((1, D_MODEL), fix),
                  pl.BlockSpec((1, D_MODEL), fix), pl.BlockSpec((D_MODEL, D_MODEL), fix)],
        out_specs=(pl.BlockSpec((tm, D_MODEL), row), pl.BlockSpec((tm, D_MODEL), row)),
        compiler_params=_cparams(("arbitrary",)),
        name="post_mix",
    )(x, yabd, yc, wabd, wc, g, b, wq)


def _mem_heads(q, mk, mv):
    outs = []
    for h in range(N_HEADS):
        sl = slice(h * MEM_HEAD_DIM, (h + 1) * MEM_HEAD_DIM)
        s = lax.dot_general(q[:, sl], mk[:, sl], NT_DIMS, preferred_element_type=F32)
        p = jnp.exp(s - jnp.max(s, axis=-1, keepdims=True))
        l = jnp.sum(p, axis=-1, keepdims=True)
        outs.append(jnp.dot(p.astype(BF16), mv[:, sl], preferred_element_type=F32) / l)
    return outs


def _mem_prompt_body(q_ref, mk_ref, mv_ref, o_ref):
    outs = _mem_heads(q_ref[...], mk_ref[...].astype(BF16), mv_ref[...].astype(BF16))
    for h in range(N_HEADS):
        o_ref[:, h * MEM_HEAD_DIM:(h + 1) * MEM_HEAD_DIM] = outs[h].astype(BF16)


def _mem_prompt(q, mk, mv):
    n = q.shape[0]
    tm = _pick(n, (512, 256, 128))
    row = lambda i: (i, 0)
    fix = lambda i: (0, 0)
    return pl.pallas_call(
        _mem_prompt_body,
        out_shape=jax.ShapeDtypeStruct((n, D_MODEL), BF16),
        grid=(n // tm,),
        in_specs=[pl.BlockSpec((tm, D_MODEL), row), pl.BlockSpec((N_MEM, D_MODEL), fix),
                  pl.BlockSpec((N_MEM, D_MODEL), fix)],
        out_specs=pl.BlockSpec((tm, D_MODEL), row),
        compiler_params=_cparams(("arbitrary",)),
        name="mem_prompt",
    )(q, mk, mv)


def _mem_sample_body(q_ref, mk_ref, mv_ref, o_ref, *, t_new):
    b = pl.program_id(0)
    n = q_ref.shape[0]

    @pl.when(b == 0)
    def _():
        o_ref[...] = jnp.zeros_like(o_ref)

    outs = _mem_heads(q_ref[...], mk_ref[0].astype(BF16), mv_ref[0].astype(BF16))
    mine = lax.broadcasted_iota(I32, (n, MEM_HEAD_DIM), 0) // t_new == b
    for h in range(N_HEADS):
        sl = slice(h * MEM_HEAD_DIM, (h + 1) * MEM_HEAD_DIM)
        o_ref[:, sl] = jnp.where(mine, outs[h].astype(BF16), o_ref[:, sl])


def _mem_sample(q, cache_k, cache_v, base, n_batch):
    n = q.shape[0]
    fix = lambda b: (0, 0)
    per_b = lambda b: (base + b, 0, 0)
    return pl.pallas_call(
        functools.partial(_mem_sample_body, t_new=n // n_batch),
        out_shape=jax.ShapeDtypeStruct((n, D_MODEL), BF16),
        grid=(n_batch,),
        in_specs=[pl.BlockSpec((n, D_MODEL), fix), pl.BlockSpec((1, N_MEM, D_MODEL), per_b),
                  pl.BlockSpec((1, N_MEM, D_MODEL), per_b)],
        out_specs=pl.BlockSpec((n, D_MODEL), fix),
        compiler_params=_cparams(("arbitrary",)),
        name="mem_sample",
    )(q, cache_k, cache_v)


def _post_attn_body(x1_ref, o_ref, wo_ref, g_ref, b_ref, x2_ref):
    att = jnp.dot(o_ref[...], wo_ref[...], preferred_element_type=F32)
    x2_ref[...] = _ln(DEEPNORM_ALPHA * x1_ref[...] + att, g_ref[...], b_ref[...])


def _post_attn(x1, o, wo, g, b):
    n = x1.shape[0]
    tm = _pick(n, (512, 256, 128, 64, 32, 16, 8))
    row = lambda i: (i, 0)
    fix = lambda i: (0, 0)
    return pl.pallas_call(
        _post_attn_body,
        out_shape=jax.ShapeDtypeStruct((n, D_MODEL), F32),
        grid=(n // tm,),
        in_specs=[pl.BlockSpec((tm, D_MODEL), row), pl.BlockSpec((tm, D_MODEL), row),
                  pl.BlockSpec((D_MODEL, D_MODEL), fix), pl.BlockSpec((1, D_MODEL), fix),
                  pl.BlockSpec((1, D_MODEL), fix)],
        out_specs=pl.BlockSpec((tm, D_MODEL), row),
        compiler_params=_cparams(("arbitrary",)),
        name="post_attn",
    )(x1, o, wo, g, b)


def _linear2_body(x_ref, w1_ref, w2_ref, o1_ref, o2_ref):
    xb = x_ref[...].astype(BF16)
    o1_ref[...] = jnp.dot(xb, w1_ref[...], preferred_element_type=F32)
    o2_ref[...] = jnp.dot(xb, w2_ref[...], preferred_element_type=F32)


def _linear2(x, w1, w2):
    n, d = x.shape
    fix = lambda i: (0, 0)
    o = jax.ShapeDtypeStruct((n, w1.shape[1]), F32)
    return pl.pallas_call(
        _linear2_body, out_shape=(o, o), grid=(1,),
        in_specs=[pl.BlockSpec((n, d), fix), pl.BlockSpec(w1.shape, fix), pl.BlockSpec(w2.shape, fix)],
        out_specs=(pl.BlockSpec((n, w1.shape[1]), fix),) * 2,
        compiler_params=_cparams(("arbitrary",)),
        name="mem_kv",
    )(x, w1, w2)


def _split_bf16(x):
    hi = x.astype(BF16)
    return hi, (x - hi.astype(F32)).astype(BF16)


def _router_body(x_ref, wt_ref, rb_ref, eidx_ref, rank_ref, gate_ref, cnt_ref, carry_ref, *, tm):
    i = pl.program_id(0)

    @pl.when(i == 0)
    def _():
        carry_ref[...] = jnp.zeros_like(carry_ref)

    xh, xl = _split_bf16(x_ref[...])
    wh, wl = _split_bf16(wt_ref[...])
    dg = lambda a, b: lax.dot_general(a, b, NT_DIMS, preferred_element_type=F32)
    logits = dg(wh, xh) + dg(wh, xl) + dg(wl, xh) + rb_ref[...]

    eio = lax.broadcasted_iota(I32, (N_EXPERTS, tm), 0).astype(F32)
    l = logits
    vals, hots = [], []
    for k in range(TOP_K):
        m = jnp.max(l, axis=0, keepdims=True)
        idx = jnp.min(jnp.where(l == m, eio, float(N_EXPERTS)), axis=0, keepdims=True)
        hot = eio == idx
        eidx_ref[k:k + 1, :] = idx.astype(I32)
        vals.append(m)
        hots.append(hot)
        l = jnp.where(hot, -jnp.inf, l)
    ex = [jnp.exp(v - vals[0]) for v in vals]
    den = ex[0] + ex[1] + ex[2] + ex[3]
    for k in range(TOP_K):
        gate_ref[k:k + 1, :] = ex[k] / den

    cnt = jnp.zeros((N_EXPERTS, tm), F32)
    for k in range(TOP_K):
        cnt = cnt + jnp.where(hots[k], 1.0, 0.0)
    tr = lax.broadcasted_iota(I32, (tm, tm), 0)
    tc = lax.broadcasted_iota(I32, (tm, tm), 1)
    before = jnp.where(tr < tc, 1.0, 0.0).astype(BF16)
    excl = jnp.dot(cnt.astype(BF16), before, preferred_element_type=F32) + carry_ref[...]
    for k in range(TOP_K):
        r = jnp.sum(jnp.where(hots[k], excl, 0.0), axis=0, keepdims=True)
        rank_ref[k:k + 1, :] = r.astype(I32)
    carry_ref[...] = carry_ref[...] + jnp.sum(cnt, axis=1, keepdims=True)
    cnt_ref[...] = carry_ref[...]


def _router(x, router_wt, router_b):
    n = x.shape[0]
    tm = _pick(n, (384, 256, 128))
    col = lambda i: (0, i)
    fix = lambda i: (0, 0)
    return pl.pallas_call(
        functools.partial(_router_body, tm=tm),
        out_shape=(jax.ShapeDtypeStruct((TOP_K, n), I32), jax.ShapeDtypeStruct((TOP_K, n), I32),
                   jax.ShapeDtypeStruct((TOP_K, n), F32), jax.ShapeDtypeStruct((N_EXPERTS, 1), F32)),
        grid=(n // tm,),
        in_specs=[pl.BlockSpec((tm, D_MODEL), lambda i: (i, 0)), pl.BlockSpec((N_EXPERTS, D_MODEL), fix),
                  pl.BlockSpec((N_EXPERTS, 1), fix)],
        out_specs=(pl.BlockSpec((TOP_K, tm), col), pl.BlockSpec((TOP_K, tm), col),
                   pl.BlockSpec((TOP_K, tm), col), pl.BlockSpec((N_EXPERTS, 1), fix)),
        scratch_shapes=[pltpu.VMEM((N_EXPERTS, 1), F32)],
        compiler_params=_cparams(("arbitrary",)),
        name="router",
    )(x, router_wt, router_b)


def _row_copy(src, s, dst, d, sem):
    return pltpu.make_async_copy(src.at[pl.ds(s, 1)], dst.at[pl.ds(d, 1)], sem)


def _dispatch_body(dest_ref, x_ref, xs_in, xs_out, sem, *, tm):
    del xs_in
    base = pl.program_id(0) * tm * TOP_K

    def issue(r, carry):
        for k in range(TOP_K):
            _row_copy(x_ref, r, xs_out, dest_ref[base + r * TOP_K + k], sem).start()
        return carry

    lax.fori_loop(0, tm, issue, 0)

    def drain(r, carry):
        for k in range(TOP_K):
            _row_copy(x_ref, 0, xs_out, 0, sem).wait()
        return carry

    lax.fori_loop(0, tm, drain, 0)


def _dispatch(dest_flat, x, n_slots):
    n = x.shape[0]
    tm = _pick(n, (384, 256, 128, 64, 32, 16, 8))
    grid_spec = pltpu.PrefetchScalarGridSpec(
        num_scalar_prefetch=1,
        grid=(n // tm,),
        in_specs=[pl.BlockSpec((tm, D_MODEL), lambda i, d: (i, 0)), pl.BlockSpec(memory_space=pl.ANY)],
        out_specs=pl.BlockSpec(memory_space=pl.ANY),
        scratch_shapes=[pltpu.SemaphoreType.DMA(())],
    )
    return pl.pallas_call(
        functools.partial(_dispatch_body, tm=tm),
        out_shape=jax.ShapeDtypeStruct((n_slots, D_MODEL), F32),
        grid_spec=grid_spec,
        input_output_aliases={2: 0},
        compiler_params=_cparams(("arbitrary",)),
        name="moe_dispatch",
    )(dest_flat, x, jnp.zeros((n_slots, D_MODEL), F32))


def _expert_body(be_ref, nu_ref, xs_ref, wu_ref, bu_ref, wd_ref, bd_ref, ys_ref, wub, wdb):
    i = pl.program_id(0)
    e = be_ref[i]
    prev = be_ref[jnp.maximum(i - 1, 0)]

    @pl.when((i == 0) | (e != prev))
    def _():
        wub[...] = wu_ref[0].astype(BF16)
        wdb[...] = wd_ref[0].astype(BF16)

    @pl.when(i < nu_ref[0])
    def _():
        h = jnp.dot(xs_ref[...].astype(BF16), wub[...], preferred_element_type=F32) + bu_ref[0]
        g = jnp.minimum(h[:, :D_FF], SWIGLU_LIMIT)
        lin = jnp.clip(h[:, D_FF:], -SWIGLU_LIMIT, SWIGLU_LIMIT)
        a = g * _sigmoid(SWIGLU_ALPHA * g) * (lin + 1.0)
        ys_ref[...] = jnp.dot(a.astype(BF16), wdb[...], preferred_element_type=F32) + bd_ref[0]

    @pl.when(i >= nu_ref[0])
    def _():
        ys_ref[...] = jnp.zeros_like(ys_ref)


def _experts(block_expert, n_used, xs, w_up, b_up, w_down, b_down, layer):
    n_slots = xs.shape[0]
    bm = EXPERT_BLOCK
    eb = layer * N_EXPERTS
    grid_spec = pltpu.PrefetchScalarGridSpec(
        num_scalar_prefetch=2,
        grid=(n_slots // bm,),
        in_specs=[pl.BlockSpec((bm, D_MODEL), lambda i, be, nu: (i, 0)),
                  pl.BlockSpec((1, D_MODEL, 2 * D_FF), lambda i, be, nu: (eb + be[i], 0, 0)),
                  pl.BlockSpec((1, 1, 2 * D_FF), lambda i, be, nu: (eb + be[i], 0, 0)),
                  pl.BlockSpec((1, D_FF, D_MODEL), lambda i, be, nu: (eb + be[i], 0, 0)),
                  pl.BlockSpec((1, 1, D_MODEL), lambda i, be, nu: (eb + be[i], 0, 0))],
        out_specs=pl.BlockSpec((bm, D_MODEL), lambda i, be, nu: (i, 0)),
        scratch_shapes=[pltpu.VMEM((D_MODEL, 2 * D_FF), BF16), pltpu.VMEM((D_FF, D_MODEL), BF16)],
    )
    return pl.pallas_call(
        _expert_body,
        out_shape=jax.ShapeDtypeStruct((n_slots, D_MODEL), F32),
        grid_spec=grid_spec,
        compiler_params=_cparams(("arbitrary",)),
        name="moe_experts",
    )(block_expert, n_used, xs, w_up, b_up, w_down, b_down)


def _combine_body(dest_ref, ys_hbm, x2_ref, gt_ref, g_ref, b_ref, o_ref, buf, sem, *, tm):
    base = pl.program_id(0) * tm * TOP_K

    def issue(r, carry):
        for k in range(TOP_K):
            _row_copy(ys_hbm, dest_ref[base + r * TOP_K + k], buf.at[k], r, sem).start()
        return carry

    lax.fori_loop(0, tm, issue, 0)

    def drain(r, carry):
        for k in range(TOP_K):
            _row_copy(ys_hbm, 0, buf.at[k], 0, sem).wait()
        return carry

    lax.fori_loop(0, tm, drain, 0)
    gt = gt_ref[...]
    ff = gt[:, 0:1] * buf[0]
    for k in range(1, TOP_K):
        ff = ff + gt[:, k:k + 1] * buf[k]
    o_ref[...] = _ln(DEEPNORM_ALPHA * x2_ref[...] + ff, g_ref[...], b_ref[...])


def _combine(dest_flat, ys, x2, gates_t, g, b):
    n = x2.shape[0]
    tm = _pick(n, (384, 256, 128, 64, 32, 16, 8))
    row = lambda i, d: (i, 0)
    fix = lambda i, d: (0, 0)
    grid_spec = pltpu.PrefetchScalarGridSpec(
        num_scalar_prefetch=1,
        grid=(n // tm,),
        in_specs=[pl.BlockSpec(memory_space=pl.ANY), pl.BlockSpec((tm, D_MODEL), row),
                  pl.BlockSpec((tm, TOP_K), row), pl.BlockSpec((1, D_MODEL), fix),
                  pl.BlockSpec((1, D_MODEL), fix)],
        out_specs=pl.BlockSpec((tm, D_MODEL), row),
        scratch_shapes=[pltpu.VMEM((TOP_K, tm, D_MODEL), F32), pltpu.SemaphoreType.DMA(())],
    )
    return pl.pallas_call(
        functools.partial(_combine_body, tm=tm),
        out_shape=jax.ShapeDtypeStruct((n, D_MODEL), F32),
        grid_spec=grid_spec,
        compiler_params=_cparams(("arbitrary",)),
        name="moe_combine",
    )(dest_flat, ys, x2, gates_t, g, b)


def _moe(x2, router_w, router_b, w_up, b_up, w_down, b_down, g, b, layer):
    n = x2.shape[0]
    bm = EXPERT_BLOCK
    eidx, rank, gates, counts = _router(x2, router_w.T, router_b.reshape(N_EXPERTS, 1))
    counts = counts[:, 0].astype(I32)
    padded = (counts + bm - 1) // bm * bm
    pend = jnp.cumsum(padded)
    pstart = pend - padded
    dest = (pstart[eidx] + rank).T.reshape(-1)
    n_blocks = -(-(n * TOP_K) // bm) + N_EXPERTS
    block_expert = jnp.minimum(
        jnp.searchsorted(pend, jnp.arange(n_blocks, dtype=I32) * bm, side="right"), N_EXPERTS - 1
    ).astype(I32)
    n_used = (pend[-1] // bm).astype(I32).reshape(1)
    xs = _dispatch(dest, x2, n_blocks * bm)
    ys = _experts(block_expert, n_used, xs, w_up, b_up, w_down, b_down, layer)
    return _combine(dest, ys, x2, gates.T, g, b)


def kernel(x_prompt, x_sample, cache_sb_k, cache_sb_v, state_pool, state_conv, cache_mem_k, cache_mem_v, page_table, mem_prompt, w_in, w_o, pool_w, pool_scale, gmlp_ln_g, gmlp_ln_b, gmlp_ws, gmlp_bs, sb_bias, conv_w, conv_b, conv_ln_g, conv_ln_b, mem_wq, mem_wk, mem_wv, mem_wo, ln_g, ln_b, router_w, router_b, expert_w_up, expert_b_up, expert_w_down, expert_b_down):
    B_p, T_p, D = x_prompt.shape
    B_s, T_s, _ = x_sample.shape
    assert B_p == 1 and D == D_MODEL and T_p % CHUNK == 0 and T_s <= 8
    G = D_GROUP
    depth = w_in.shape[0]
    n_phys = cache_sb_k.shape[1]
    n_pages = page_table.shape[1]
    past_len = n_pages * PAGE
    n_p, n_s = B_p * T_p, B_s * T_s

    xp = x_prompt.reshape(n_p, D)
    xs = x_sample.reshape(n_s, D)
    cache_k = cache_sb_k.reshape(depth * n_phys, PAGE, G)
    cache_v = cache_sb_v.reshape(depth * n_phys, PAGE, G)
    cmem_k = cache_mem_k.reshape(depth * B_s, N_MEM, D)
    cmem_v = cache_mem_v.reshape(depth * B_s, N_MEM, D)
    w_up = expert_w_up.reshape(depth * N_EXPERTS, D, 2 * D_FF)
    w_down = expert_w_down.reshape(depth * N_EXPERTS, D_FF, D)
    b_up = expert_b_up.reshape(depth * N_EXPERTS, 1, 2 * D_FF)
    b_down = expert_b_down.reshape(depth * N_EXPERTS, 1, D)
    pool_zero = jnp.zeros((B_p, POOL_PAD, G), F32)
    conv_zero = jnp.zeros((B_p, CONV_PAD, G), F32)
    head_mask = (np.arange(G)[None, :] // HEAD_DIM == np.arange(N_HEADS)[:, None]).astype(np.float32)

    outs = {k: [] for k in ("sbk_p", "sbv_p", "sbk_s", "sbv_s", "pool_p", "pool_s", "conv_p", "conv_s",
                            "gv_s", "mk_p", "mv_p")}
    row = lambda v: v.reshape(1, -1)
    for l in range(depth):
        w_in_bf = w_in[l].astype(BF16)
        w_o_l = w_o[l].astype(BF16)
        wabd = jnp.concatenate([w_o_l[:2 * G], w_o_l[3 * G:]], axis=0)
        wc = w_o_l[2 * G:3 * G]
        wq = mem_wq[l].astype(BF16)
        wo = mem_wo[l].astype(BF16)
        pool_bd = jax.scipy.linalg.block_diag(*[pool_w[l, g] for g in range(len(POOL_WINDOWS))]).astype(BF16)
        bs_tile = jnp.repeat(gmlp_bs[l].T, HEAD_DIM, axis=1)
        ws_small = jnp.repeat(jnp.transpose(gmlp_ws[l, :, :T_s, :T_s], (1, 2, 0)), HEAD_DIM, axis=2)
        mix_args = (pool_bd, row(pool_scale[l]))
        conv_args = (conv_w[l], row(conv_b[l]), row(conv_ln_g[l]), row(conv_ln_b[l]))

        mem_k, mem_v = _linear2(mem_prompt.reshape(N_MEM, D), mem_wk[l].astype(BF16), mem_wv[l].astype(BF16))
        outs["mk_p"].append(mem_k.reshape(B_p, N_MEM, N_HEADS, MEM_HEAD_DIM))
        outs["mv_p"].append(mem_v.reshape(B_p, N_MEM, N_HEADS, MEM_HEAD_DIM))

        p, gu, vn, q, k, sv, kb, vb, glu = _in_proj(xp, w_in_bf, row(gmlp_ln_g[l]), row(gmlp_ln_b[l]))
        outs["sbk_p"].append(k.reshape(B_p, T_p, N_HEADS, HEAD_DIM))
        outs["sbv_p"].append(sv.reshape(B_p, T_p, N_HEADS, HEAD_DIM))
        r3 = lambda v: v.reshape(B_p, T_p, -1)
        yabd, pool_new, conv_new = _mixers(pool_zero, conv_zero, r3(p), r3(gu), r3(vn), r3(glu), *mix_args,
                                           gmlp_ws[l], bs_tile, *conv_args, pos0=0)
        outs["pool_p"].append(pool_new)
        outs["conv_p"].append(conv_new)
        yc = _sb_prompt(q, kb, vb, sb_bias[l])
        x1, qm = _post_mix(xp, yabd.reshape(n_p, 3 * G), yc, wabd, wc, row(ln_g[l, 0]), row(ln_b[l, 0]), wq)
        om = _mem_prompt(qm, mem_k, mem_v)
        x2p = _post_attn(x1, om, wo, row(ln_g[l, 1]), row(ln_b[l, 1]))

        p, gu, vn, q, k, sv, kb, vb, glu = _in_proj(xs, w_in_bf, row(gmlp_ln_g[l]), row(gmlp_ln_b[l]))
        outs["sbk_s"].append(k.reshape(B_s, T_s, N_HEADS, HEAD_DIM))
        outs["sbv_s"].append(sv.reshape(B_s, T_s, N_HEADS, HEAD_DIM))
        outs["gv_s"].append(vn.reshape(B_s, T_s, G))
        r3 = lambda v: v.reshape(B_s, T_s, -1)
        yabd, pool_new, conv_new = _mixers(state_pool[l], state_conv[l], r3(p), r3(gu), r3(vn), r3(glu),
                                           *mix_args, ws_small, bs_tile, *conv_args, pos0=past_len)
        outs["pool_s"].append(pool_new)
        outs["conv_s"].append(conv_new)
        qs = (r3(q)[:, None, :, :] * head_mask[None, :, None, :]).reshape(B_s, N_HEADS * T_s, G).astype(BF16)
        pad = ((0, 0), (0, PAGE - T_s), (0, 0))
        acc = _sb_sample(qs, jnp.pad(r3(kb), pad), jnp.pad(r3(vb), pad), cache_k, cache_v, page_table,
                         l * n_phys, sb_bias[l])
        yc = jnp.sum(acc.reshape(B_s, N_HEADS, T_s, G) * head_mask[None, :, None, :], axis=1)
        yc = yc.reshape(n_s, G).astype(BF16)
        x1, qm = _post_mix(xs, yabd.reshape(n_s, 3 * G), yc, wabd, wc, row(ln_g[l, 0]), row(ln_b[l, 0]), wq)
        om = _mem_sample(qm, cmem_k, cmem_v, l * B_s, B_s)
        x2s = _post_attn(x1, om, wo, row(ln_g[l, 1]), row(ln_b[l, 1]))

        x3 = _moe(jnp.concatenate([x2p, x2s], axis=0), router_w[l], router_b[l], w_up, b_up, w_down, b_down,
                  row(ln_g[l, 2]), row(ln_b[l, 2]), l)
        xp, xs = x3[:n_p], x3[n_p:]

    st = lambda k: jnp.stack(outs[k])
    return (xp.reshape(B_p, T_p, D), xs.reshape(B_s, T_s, D),
            st("sbk_p"), st("sbv_p"), st("sbk_s"), st("sbv_s"),
            st("pool_p"), st("pool_s"), st("conv_p"), st("conv_s"),
            st("gv_s"), st("mk_p"), st("mv_p"))
```
